```python
import jax, jax.numpy as jnp
from jax import lax

D_MODEL = 1024
BATCH = 4
SEQ = 8192
DEPTH = 1
DEC_BATCH = 32
DEC_SEQ = 16
PAST_LEN = 4096

CHUNK = 64
CONV_CH = 512
CONV_WIDTH = 31
N_HEADS = 8
N_KV_HEADS = 2
HEAD_DIM = 64
Q_PER_KV = N_HEADS // N_KV_HEADS
ATTN_WIDTH = N_HEADS * HEAD_DIM
KV_WIDTH = N_KV_HEADS * HEAD_DIM
MIX_WIDTH = CONV_CH + ATTN_WIDTH
IN_COLS = 2 * CONV_CH + ATTN_WIDTH + 2 * KV_WIDTH
SPLITS = [CONV_CH, 2 * CONV_CH, 2 * CONV_CH + ATTN_WIDTH, 2 * CONV_CH + ATTN_WIDTH + KV_WIDTH]
WINDOW = 128
WINDOW_CHUNKS = WINDOW // CHUNK
ROPE_DIM = HEAD_DIM // 4
ROPE_THETA = 500000.0
D_FF = 2816
EPS = 1e-6

kernel_name = "hymba_conformer_swa_sink_stream_step"


def rms_norm(x, g):
    xf = x.astype(jnp.float32)
    y = xf * lax.rsqrt(jnp.mean(xf * xf, axis=-1, keepdims=True) + EPS)
    return (y * g.astype(jnp.float32)).astype(x.dtype)


def layer_norm(x, g, b):
    xf = x.astype(jnp.float32)
    mu = jnp.mean(xf, axis=-1, keepdims=True)
    xc = xf - mu
    y = xc * lax.rsqrt(jnp.mean(xc * xc, axis=-1, keepdims=True) + EPS)
    return (y * g.astype(jnp.float32) + b.astype(jnp.float32)).astype(x.dtype)


def swiglu_ffn(x, g, w_up, w_down):
    a, b = jnp.split(rms_norm(x, g) @ w_up, 2, axis=-1)
    return (jax.nn.silu(a) * b) @ w_down


def partial_rope(x, pos):
    half = ROPE_DIM // 2
    inv = jnp.power(jnp.float32(ROPE_THETA), -jnp.arange(half, dtype=jnp.float32) / half)
    ang = pos.astype(jnp.float32)[:, None] * inv[None, :]
    cos = jnp.cos(ang)[:, None, :]
    sin = jnp.sin(ang)[:, None, :]
    xf = x.astype(jnp.float32)
    x1, x2 = xf[..., :half], xf[..., half:ROPE_DIM]
    out = jnp.concatenate([x1 * cos - x2 * sin, x2 * cos + x1 * sin, xf[..., ROPE_DIM:]], axis=-1)
    return out.astype(x.dtype)


def sink_attend(q, k, v, key_valid, sinks):
    s = jnp.einsum('bnqhgd,bnkhd->bnhgqk', q, k, preferred_element_type=jnp.float32) * (HEAD_DIM ** -0.5)
    if key_valid is not None:
        s = jnp.where(key_valid[None, :, None, None, None, :], s, -jnp.inf)
    sink = jnp.broadcast_to(sinks.astype(jnp.float32).reshape(1, 1, N_KV_HEADS, Q_PER_KV, 1, 1), s.shape[:-1] + (1,))
    p = jax.nn.softmax(jnp.concatenate([s, sink], axis=-1), axis=-1)[..., :-1]
    return jnp.einsum('bnhgqk,bnkhd->bnqhgd', p.astype(v.dtype), v)


def banded_window_attention(q, k, v, sinks):
    B, T = q.shape[:2]
    n_c = T // CHUNK
    qb = q.reshape(B, n_c, CHUNK, N_KV_HEADS, Q_PER_KV, HEAD_DIM)

    def band(t):
        tc = t.reshape(B, n_c, CHUNK, N_KV_HEADS, HEAD_DIM)
        tp = jnp.pad(tc, ((0, 0), (WINDOW_CHUNKS, 0), (0, 0), (0, 0), (0, 0)))
        return jnp.concatenate([tp[:, j:j + n_c] for j in range(WINDOW_CHUNKS + 1)], axis=2)

    kb, vb = band(k), band(v)
    key_chunk = jnp.arange(n_c)[:, None] - WINDOW_CHUNKS + jnp.arange(WINDOW_CHUNKS + 1)[None, :]
    valid = jnp.repeat(key_chunk >= 0, CHUNK, axis=1)
    o = sink_attend(qb, kb, vb, valid, sinks)
    return o.reshape(B, T, ATTN_WIDTH)


def conv_tail(u_hist, w_dw, b_dw, g_cn, b_cn):
    y = lax.conv_general_dilated(u_hist, w_dw[:, None, :], window_strides=(1,), padding='VALID',
                                 dimension_numbers=('NWC', 'WIO', 'NWC'), feature_group_count=CONV_CH)
    return jax.nn.silu(layer_norm(y + b_dw, g_cn, b_cn))


def encoder_layer(x, pos, conv_prev, k_prev, v_prev, g_ff1, w_ff1_in, w_ff1_out, g_mix, w_in, g_q, g_k,
                  sinks, w_dw, b_dw, g_cn, b_cn, w_out, g_ff2, w_ff2_in, w_ff2_out):
    B, T, _ = x.shape
    x = x + 0.5 * swiglu_ffn(x, g_ff1, w_ff1_in, w_ff1_out)
    z = rms_norm(x, g_mix) @ w_in
    cv, cg, q, k, v = jnp.split(z, SPLITS, axis=-1)
    u = cv * jax.nn.sigmoid(cg)
    if conv_prev is None:
        conv_prev = jnp.zeros((B, CONV_WIDTH - 1, CONV_CH), u.dtype)
    u_hist = jnp.concatenate([conv_prev, u], axis=1)
    c_out = conv_tail(u_hist, w_dw, b_dw, g_cn, b_cn)
    new_conv = u_hist[:, -(CONV_WIDTH - 1):]
    q = partial_rope(rms_norm(q.reshape(B, T, N_HEADS, HEAD_DIM), g_q), pos)
    k = partial_rope(rms_norm(k.reshape(B, T, N_KV_HEADS, HEAD_DIM), g_k), pos)
    v = v.reshape(B, T, N_KV_HEADS, HEAD_DIM)
    if k_prev is None:
        a_out = banded_window_attention(q, k, v, sinks)
        k_all, v_all = k, v
    else:
        k_all = jnp.concatenate([k_prev, k], axis=1)
        v_all = jnp.concatenate([v_prev, v], axis=1)
        qs = q.reshape(B, 1, T, N_KV_HEADS, Q_PER_KV, HEAD_DIM)
        a_out = sink_attend(qs, k_all[:, None], v_all[:, None], None, sinks).reshape(B, T, ATTN_WIDTH)
    x = x + jnp.concatenate([c_out, a_out], axis=-1) @ w_out
    x = x + 0.5 * swiglu_ffn(x, g_ff2, w_ff2_in, w_ff2_out)
    return x, new_conv, k_all[:, -WINDOW:], v_all[:, -WINDOW:]


def setup_inputs(seed: int = 0) -> dict:
    key = jax.random.key(seed)
    ks = jax.random.split(key, 24)
    f32 = jnp.float32
    nrm = lambda k, shape, s: jax.random.normal(k, shape, f32) * s
    L = DEPTH
    return {
        "x_prompt": nrm(ks[0], (BATCH, SEQ, D_MODEL), 1.0),
        "x_sample": nrm(ks[1], (DEC_BATCH, DEC_SEQ, D_MODEL), 1.0),
        "state_conv": nrm(ks[2], (L, DEC_BATCH, CONV_WIDTH - 1, CONV_CH), 0.5),
        "cache_k_win": nrm(ks[3], (L, DEC_BATCH, WINDOW, N_KV_HEADS, HEAD_DIM), 1.0),
        "cache_v_win": nrm(ks[4], (L, DEC_BATCH, WINDOW, N_KV_HEADS, HEAD_DIM), 1.0),
        "g_ff1": 1.0 + nrm(ks[5], (L, D_MODEL), 0.02),
        "w_ff1_in": nrm(ks[6], (L, D_MODEL, 2 * D_FF), D_MODEL ** -0.5),
        "w_ff1_out": nrm(ks[7], (L, D_FF, D_MODEL), D_FF ** -0.5),
        "g_mix": 1.0 + nrm(ks[8], (L, D_MODEL), 0.02),
        "w_in": nrm(ks[9], (L, D_MODEL, IN_COLS), D_MODEL ** -0.5),
        "g_q": 1.0 + nrm(ks[10], (L, HEAD_DIM), 0.02),
        "g_k": 1.0 + nrm(ks[11], (L, HEAD_DIM), 0.02),
        "sinks": nrm(ks[12], (L, N_HEADS), 0.5),
        "w_dw": nrm(ks[13], (L, CONV_WIDTH, CONV_CH), CONV_WIDTH ** -0.5),
        "b_dw": nrm(ks[14], (L, CONV_CH), 0.02),
        "g_cn": 1.0 + nrm(ks[15], (L, CONV_CH), 0.02),
        "b_cn": nrm(ks[16], (L, CONV_CH), 0.02),
        "w_out": nrm(ks[17], (L, MIX_WIDTH, D_MODEL), MIX_WIDTH ** -0.5),
        "g_ff2": 1.0 + nrm(ks[18], (L, D_MODEL), 0.02),
        "w_ff2_in": nrm(ks[19], (L, D_MODEL, 2 * D_FF), D_MODEL ** -0.5),
        "w_ff2_out": nrm(ks[20], (L, D_FF, D_MODEL), D_FF ** -0.5),
    }


def reference(x_prompt, x_sample, state_conv, cache_k_win, cache_v_win, g_ff1, w_ff1_in, w_ff1_out,
              g_mix, w_in, g_q, g_k, sinks, w_dw, b_dw, g_cn, b_cn, w_out, g_ff2, w_ff2_in, w_ff2_out):
    pos_p = jnp.arange(x_prompt.shape[1], dtype=jnp.int32)
    pos_s = PAST_LEN + jnp.arange(x_sample.shape[1], dtype=jnp.int32)
    hp, hs = x_prompt, x_sample
    cp_l, kp_l, vp_l, cs_l, kss_l, vs_l = [], [], [], [], [], []
    for l in range(DEPTH):
        w = (g_ff1[l], w_ff1_in[l], w_ff1_out[l], g_mix[l], w_in[l], g_q[l], g_k[l], sinks[l], w_dw[l],
             b_dw[l], g_cn[l], b_cn[l], w_out[l], g_ff2[l], w_ff2_in[l], w_ff2_out[l])
        hp, cp, kp, vp = encoder_layer(hp, pos_p, None, None, None, *w)
        hs, cs, kss, vs = encoder_layer(hs, pos_s, state_conv[l], cache_k_win[l], cache_v_win[l], *w)
        cp_l.append(cp); kp_l.append(kp); vp_l.append(vp)
        cs_l.append(cs); kss_l.append(kss); vs_l.append(vs)
    conv_state_prompt = jnp.stack(cp_l)
    k_win_prompt = jnp.stack(kp_l)
    v_win_prompt = jnp.stack(vp_l)
    conv_state_sample = jnp.stack(cs_l)
    k_win_sample = jnp.stack(kss_l)
    v_win_sample = jnp.stack(vs_l)
    return (hp, hs, conv_state_prompt, k_win_prompt, v_win_prompt, conv_state_sample, k_win_sample, v_win_sample)
```

```python
import functools

import jax
import jax.numpy as jnp
from jax.experimental import pallas as pl
from jax.experimental.pallas import tpu as pltpu

D_MODEL = 1024
D_FF = 2816
CONV_CH = 512
CONV_WIDTH = 31
N_HEADS = 8
N_KV_HEADS = 2
HEAD_DIM = 64
Q_PER_KV = N_HEADS // N_KV_HEADS
ATTN_WIDTH = N_HEADS * HEAD_DIM
KV_WIDTH = N_KV_HEADS * HEAD_DIM
MIX_WIDTH = CONV_CH + ATTN_WIDTH
IN_COLS = 2 * CONV_CH + ATTN_WIDTH + 2 * KV_WIDTH
WINDOW = 128
CHUNK = 64
ROPE_DIM = HEAD_DIM // 4
ROPE_THETA = 500000.0
PAST_LEN = 4096
EPS = 1e-6

V7X_LANES = 128
V7X_SUBLANES = 8
V7X_MXU_DIM = 256
V7X_VMEM_BYTES = 64 * 1024 * 1024

HIST_ROWS = 32
HIST_OFF = HIST_ROWS - (CONV_WIDTH - 1)
FF_CHUNK = V7X_MXU_DIM
CONV_ROWS = 64

F32 = jnp.float32
BF16 = jnp.bfloat16


def _dot(a, b):
    return jnp.dot(a, b, preferred_element_type=F32)


def _sigmoid(x):
    return 1.0 / (1.0 + jnp.exp(-x))


def _rms(x, g):
    return x * jax.lax.rsqrt(jnp.mean(x * x, axis=-1, keepdims=True) + EPS) * g


def _swiglu(x, g_ref, wup_ref, wdn_ref):
    xn = _rms(x, g_ref[...]).astype(BF16)
    acc = None
    for c in range(D_FF // FF_CHUNK):
        lo = c * FF_CHUNK
        a = _dot(xn, wup_ref[:, lo:lo + FF_CHUNK])
        b = _dot(xn, wup_ref[:, D_FF + lo:D_FF + lo + FF_CHUNK])
        h = (a * _sigmoid(a) * b).astype(BF16)
        d = _dot(h, wdn_ref[lo:lo + FF_CHUNK, :])
        acc = d if acc is None else acc + d
    return x + 0.5 * acc


def _ffn_post_kernel(x_ref, g_ref, wup_ref, wdn_ref, gmix_ref, win_ref, x1_ref, z_ref):
    x1 = _swiglu(x_ref[...], g_ref, wup_ref, wdn_ref)
    x1_ref[...] = x1
    z_ref[...] = _dot(_rms(x1, gmix_ref[...]).astype(BF16), win_ref[...])


def _ffn_pre_kernel(x_ref, mix_ref, wout_ref, g_ref, wup_ref, wdn_ref, y_ref):
    x2 = x_ref[...] + _dot(mix_ref[...], wout_ref[...])
    y_ref[...] = _swiglu(x2, g_ref, wup_ref, wdn_ref)


def _resident(shape):
    return pl.BlockSpec(shape, lambda i: (0,) * len(shape), pipeline_mode=pl.Buffered(1))


def _ffn_tile(n_tok):
    tm = min(512, n_tok)
    assert n_tok % tm == 0
    return tm


def _ffn_post_call(x, g, wup, wdn, gmix, win):
    n = x.shape[0]
    tm = _ffn_tile(n)
    row = lambda w: pl.BlockSpec((tm, w), lambda i: (i, 0))
    return pl.pallas_call(
        _ffn_post_kernel,
        grid=(n // tm,),
        in_specs=[row(D_MODEL), _resident((1, D_MODEL)), _resident((D_MODEL, 2 * D_FF)),
                  _resident((D_FF, D_MODEL)), _resident((1, D_MODEL)), _resident((D_MODEL, IN_COLS))],
        out_specs=[row(D_MODEL), row(IN_COLS)],
        out_shape=[jax.ShapeDtypeStruct((n, D_MODEL), F32), jax.ShapeDtypeStruct((n, IN_COLS), F32)],
        compiler_params=pltpu.CompilerParams(dimension_semantics=("arbitrary",),
                                             vmem_limit_bytes=V7X_VMEM_BYTES * 7 // 8),
        name="ffn_post",
    )(x, g, wup, wdn, gmix, win)


def _ffn_pre_call(x, mix, wout, g, wup, wdn):
    n = x.shape[0]
    tm = _ffn_tile(n)
    row = lambda w: pl.BlockSpec((tm, w), lambda i: (i, 0))
    return pl.pallas_call(
        _ffn_pre_kernel,
        grid=(n // tm,),
        in_specs=[row(D_MODEL), row(MIX_WIDTH), _resident((MIX_WIDTH, D_MODEL)), _resident((1, D_MODEL)),
                  _resident((D_MODEL, 2 * D_FF)), _resident((D_FF, D_MODEL))],
        out_specs=row(D_MODEL),
        out_shape=jax.ShapeDtypeStruct((n, D_MODEL), F32),
        compiler_params=pltpu.CompilerParams(dimension_semantics=("arbitrary",),
                                             vmem_limit_bytes=V7X_VMEM_BYTES * 7 // 8),
        name="ffn_pre",
    )(x, mix, wout, g, wup, wdn)


def _head_sumsq(x):
    w = x.shape[1]
    gw = min(w, V7X_MXU_DIM)
    r = jax.lax.broadcasted_iota(jnp.int32, (gw, gw), 0) // HEAD_DIM
    c = jax.lax.broadcasted_iota(jnp.int32, (gw, gw), 1) // HEAD_DIM
    ones_bd = (r == c).astype(BF16)
    sq = x * x
    hi = sq.astype(BF16)
    lo = (sq - hi.astype(F32)).astype(BF16)
    parts = [_dot(hi[:, o:o + gw], ones_bd) + _dot(lo[:, o:o + gw], ones_bd) for o in range(0, w, gw)]
    return parts[0] if len(parts) == 1 else jnp.concatenate(parts, axis=1)


def _rope(x, cos_t, sin_t):
    lane = jax.lax.broadcasted_iota(jnp.int32, (1, V7X_LANES), 1) % HEAD_DIM
    half = ROPE_DIM // 2
    partner = jnp.where(lane < half, pltpu.roll(x, V7X_LANES - half, axis=1), pltpu.roll(x, half, axis=1))
    return x * cos_t + partner * sin_t


def _dup_heads(x):
    lane = jax.lax.broadcasted_iota(jnp.int32, (1, V7X_LANES), 1)
    sw = pltpu.roll(x, HEAD_DIM, axis=1)
    return jnp.where(lane < HEAD_DIM, x, sw), jnp.where(lane < HEAD_DIM, sw, x)


def _mixer_kernel(*refs, tm, ch, n_t, has_state):
    if has_state:
        (sinks_ref, z_ref, cos_ref, sin_ref, gq_ref, gk_ref, wdw_ref, bdw_ref, gcn_ref, bcn_ref,
         conv0_ref, k0_ref, v0_ref,
         mix_ref, cs_ref, kw_ref, vw_ref, ubuf, kbuf, vbuf, kd_buf, vd_buf, a_buf) = refs
    else:
        (sinks_ref, z_ref, cos_ref, sin_ref, gq_ref, gk_ref, wdw_ref, bdw_ref, gcn_ref, bcn_ref,
         mix_ref, cs_ref, kw_ref, vw_ref, ubuf, kbuf, vbuf, kd_buf, vd_buf, a_buf) = refs
    t = pl.program_id(1)
    win = WINDOW + ch

    @pl.when(t == 0)
    def _init():
        if has_state:
            ubuf[0:HIST_OFF, :] = jnp.zeros((HIST_OFF, CONV_CH), F32)
            ubuf[HIST_OFF:HIST_ROWS, :] = conv0_ref[0]
            k0 = k0_ref[0]
            v0 = v0_ref[0]
        else:
            ubuf[0:HIST_ROWS, :] = jnp.zeros((HIST_ROWS, CONV_CH), F32)
            k0 = jnp.zeros((WINDOW, KV_WIDTH), F32)
            v0 = k0
        kbuf[0:WINDOW, :] = k0
        vbuf[0:WINDOW, :] = v0
        for h, d in enumerate(_dup_heads(k0)):
            kd_buf[h, 0:WINDOW, :] = d.astype(BF16)
        for h, d in enumerate(_dup_heads(v0)):
            vd_buf[h, 0:WINDOW, :] = d.astype(BF16)

    cv = z_ref[0, :, 0:CONV_CH]
    cg = z_ref[0, :, CONV_CH:2 * CONV_CH]
    ubuf[HIST_ROWS:HIST_ROWS + tm, :] = cv * _sigmoid(cg)
    cs_ref[0] = ubuf[HIST_OFF + tm:HIST_ROWS + tm, :]

    rb = min(CONV_ROWS, tm)
    n_oct = (CONV_WIDTH + V7X_SUBLANES - 1) // V7X_SUBLANES
    span = rb + (n_oct - 1) * V7X_SUBLANES
    for r0 in range(0, tm, rb):
        acc = jnp.broadcast_to(bdw_ref[...], (rb, CONV_CH))
        for s in range(V7X_SUBLANES):
            rows = min(span, HIST_ROWS + tm - (r0 + HIST_OFF + s))
            shifted = ubuf[r0 + HIST_OFF + s:r0 + HIST_OFF + s + rows, :]
            for a in range(n_oct):
                j = V7X_SUBLANES * a + s
                if j < CONV_WIDTH:
                    acc = acc + shifted[V7X_SUBLANES * a:V7X_SUBLANES * a + rb, :] * wdw_ref[j:j + 1, :]
        mu = jnp.mean(acc, axis=-1, keepdims=True)
        xc = acc - mu
        y = xc * jax.lax.rsqrt(jnp.mean(xc * xc, axis=-1, keepdims=True) + EPS) * gcn_ref[...] + bcn_ref[...]
        mix_ref[0, r0:r0 + rb, 0:CONV_CH] = (y * _sigmoid(y)).astype(mix_ref.dtype)

    if n_t > 1:
        ubuf[HIST_OFF:HIST_ROWS, :] = ubuf[HIST_OFF + tm:HIST_ROWS + tm, :]

    cos_t = cos_ref[...]
    sin_t = sin_ref[...]
    q = z_ref[0, :, 2 * CONV_CH:2 * CONV_CH + ATTN_WIDTH]
    k = z_ref[0, :, 2 * CONV_CH + ATTN_WIDTH:2 * CONV_CH + ATTN_WIDTH + KV_WIDTH]
    v = z_ref[0, :, 2 * CONV_CH + ATTN_WIDTH + KV_WIDTH:IN_COLS]
    qn = q * jax.lax.rsqrt(_head_sumsq(q) * (1.0 / HEAD_DIM) + EPS)
    kn = k * jax.lax.rsqrt(_head_sumsq(k) * (1.0 / HEAD_DIM) + EPS) * gk_ref[...]
    kr = _rope(kn, cos_t, sin_t)
    kbuf[WINDOW:WINDOW + tm, :] = kr
    vbuf[WINDOW:WINDOW + tm, :] = v
    for h, d in enumerate(_dup_heads(kr)):
        kd_buf[h, WINDOW:WINDOW + tm, :] = d.astype(BF16)
    for h, d in enumerate(_dup_heads(v)):
        vd_buf[h, WINDOW:WINDOW + tm, :] = d.astype(BF16)
    kw_ref[0] = kbuf[tm:tm + WINDOW, :]
    vw_ref[0] = vbuf[tm:tm + WINDOW, :]

    lane = jax.lax.broadcasted_iota(jnp.int32, (1, V7X_LANES), 1)
    n_grp = ATTN_WIDTH // V7X_LANES
    qg = [_rope(qn[:, m * V7X_LANES:(m + 1) * V7X_LANES] * gq_ref[...], cos_t, sin_t) * (HEAD_DIM ** -0.5)
          for m in range(n_grp)]
    q_half = [[jnp.where((lane < HEAD_DIM) == (p == 0), qg[m], 0.0).astype(BF16) for p in range(2)]
              for m in range(n_grp)]
    key_chunk = jax.lax.broadcasted_iota(jnp.int32, (1, win), 1) // CHUNK

    for i in range(tm // ch):
        r0 = i * ch
        outs = []
        for h in range(N_KV_HEADS):
            heads = [Q_PER_KV * h + g for g in range(Q_PER_KV)]
            qs = jnp.concatenate([q_half[hd // 2][hd % 2][r0:r0 + ch, :] for hd in heads], axis=0)
            s = jax.lax.dot_general(qs, kd_buf[h, r0:r0 + win, :], (((1,), (1,)), ((), ())),
                                    preferred_element_type=F32)
            if not has_state:
                first_chunk = t * (tm // CHUNK) + (r0 // CHUNK) - WINDOW // CHUNK
                s = jnp.where(key_chunk + first_chunk >= 0, s, -jnp.inf)
            sink = jnp.concatenate([jnp.full((ch, 1), sinks_ref[hd], F32) for hd in heads], axis=0)
            m_row = jnp.maximum(jnp.max(s, axis=-1, keepdims=True), sink)
            p = jnp.exp(s - m_row)
            denom = jnp.sum(p, axis=-1, keepdims=True) + jnp.exp(sink - m_row)
            o = _dot(p.astype(BF16), vd_buf[h, r0:r0 + win, :]) / denom
            for g in range(0, Q_PER_KV, 2):
                outs.append(jnp.where(lane < HEAD_DIM, o[g * ch:(g + 1) * ch, :], o[(g + 1) * ch:(g + 2) * ch, :]))
        a_buf[r0:r0 + ch, :] = jnp.concatenate(outs, axis=1)
    mix_ref[0, :, CONV_CH:MIX_WIDTH] = a_buf[...].astype(mix_ref.dtype)

    if n_t > 1:
        kbuf[0:WINDOW, :] = kbuf[tm:tm + WINDOW, :]
        vbuf[0:WINDOW, :] = vbuf[tm:tm + WINDOW, :]
        for h in range(N_KV_HEADS):
            kd_buf[h, 0:WINDOW, :] = kd_buf[h, tm:tm + WINDOW, :]
            vd_buf[h, 0:WINDOW, :] = vd_buf[h, tm:tm + WINDOW, :]


def _mixer_call(z, cos_t, sin_t, sinks, gq, gk, wdw, bdw, gcn, bcn, state=None):
    bsz, t_len, _ = z.shape
    has_state = state is not None
    tm = min(512, t_len)
    ch = min(CHUNK, tm)
    assert t_len % tm == 0 and tm % ch == 0
    if has_state:
        assert t_len == tm == ch
    else:
        assert ch == CHUNK and tm >= WINDOW
    n_t = t_len // tm
    full = lambda shape: pl.BlockSpec(shape, lambda b, t: (0,) * len(shape))
    per_b = lambda r, w: pl.BlockSpec((1, r, w), lambda b, t: (b, 0, 0))
    in_specs = [pl.BlockSpec(memory_space=pltpu.SMEM),
                pl.BlockSpec((1, tm, IN_COLS), lambda b, t: (b, t, 0)),
                pl.BlockSpec((tm, V7X_LANES), lambda b, t: (t, 0)),
                pl.BlockSpec((tm, V7X_LANES), lambda b, t: (t, 0)),
                full((1, V7X_LANES)), full((1, V7X_LANES)), full((CONV_WIDTH, CONV_CH)),
                full((1, CONV_CH)), full((1, CONV_CH)), full((1, CONV_CH))]
    args = [sinks, z, cos_t, sin_t, gq, gk, wdw, bdw, gcn, bcn]
    if has_state:
        in_specs += [per_b(CONV_WIDTH - 1, CONV_CH), per_b(WINDOW, KV_WIDTH), per_b(WINDOW, KV_WIDTH)]
        args += list(state)
    kern = functools.partial(_mixer_kernel, tm=tm, ch=ch, n_t=n_t, has_state=has_state)
    return pl.pallas_call(
        kern,
        grid=(bsz, n_t),
        in_specs=in_specs,
        out_specs=[pl.BlockSpec((1, tm, MIX_WIDTH), lambda b, t: (b, t, 0)),
                   per_b(CONV_WIDTH - 1, CONV_CH), per_b(WINDOW, KV_WIDTH), per_b(WINDOW, KV_WIDTH)],
        out_shape=[jax.ShapeDtypeStruct((bsz, t_len, MIX_WIDTH), BF16),
                   jax.ShapeDtypeStruct((bsz, CONV_WIDTH - 1, CONV_CH), F32),
                   jax.ShapeDtypeStruct((bsz, WINDOW, KV_WIDTH), F32),
                   jax.ShapeDtypeStruct((bsz, WINDOW, KV_WIDTH), F32)],
        scratch_shapes=[pltpu.VMEM((HIST_ROWS + tm, CONV_CH), F32),
                        pltpu.VMEM((WINDOW + tm, KV_WIDTH), F32),
                        pltpu.VMEM((WINDOW + tm, KV_WIDTH), F32),
                        pltpu.VMEM((N_KV_HEADS, WINDOW + tm, KV_WIDTH), BF16),
                        pltpu.VMEM((N_KV_HEADS, WINDOW + tm, KV_WIDTH), BF16),
                        pltpu.VMEM((tm, ATTN_WIDTH), F32)],
        compiler_params=pltpu.CompilerParams(dimension_semantics=("arbitrary", "arbitrary"),
                                             vmem_limit_bytes=V7X_VMEM_BYTES * 3 // 4),
        name="mixer_state" if has_state else "mixer",
    )(*args)


def _rope_tables(pos):
    half = ROPE_DIM // 2
    inv = jnp.power(jnp.float32(ROPE_THETA), -jnp.arange(half, dtype=F32) / half)
    ang = pos.astype(F32)[:, None] * inv[None, :]
    cos, sin = jnp.cos(ang), jnp.sin(ang)
    rest = HEAD_DIM - ROPE_DIM
    n = pos.shape[0]
    cos_h = jnp.concatenate([cos, cos, jnp.ones((n, rest), F32)], axis=1)
    sin_h = jnp.concatenate([-sin, sin, jnp.zeros((n, rest), F32)], axis=1)
    reps = V7X_LANES // HEAD_DIM
    return jnp.tile(cos_h, (1, reps)), jnp.tile(sin_h, (1, reps))


def _layer(x, pos, state, w):
    bsz, t_len, _ = x.shape
    x2d = x.reshape(bsz * t_len, D_MODEL)
    x1, z = _ffn_post_call(x2d, w["g_ff1"], w["w_ff1_in"], w["w_ff1_out"], w["g_mix"], w["w_in"])
    cos_t, sin_t = _rope_tables(pos)
    mix, cs, kw, vw = _mixer_call(z.reshape(bsz, t_len, IN_COLS), cos_t, sin_t, w["sinks"], w["g_q"], w["g_k"],
                                  w["w_dw"], w["b_dw"], w["g_cn"], w["b_cn"], state)
    y = _ffn_pre_call(x1, mix.reshape(bsz * t_len, MIX_WIDTH), w["w_out"], w["g_ff2"], w["w_ff2_in"], w["w_ff2_out"])
    kv_shape = (bsz, WINDOW, N_KV_HEADS, HEAD_DIM)
    return y.reshape(bsz, t_len, D_MODEL), cs, kw.reshape(kv_shape), vw.reshape(kv_shape)


def kernel(x_prompt, x_sample, state_conv, cache_k_win, cache_v_win, g_ff1, w_ff1_in, w_ff1_out, g_mix, w_in, g_q, g_k, sinks, w_dw, b_dw, g_cn, b_cn, w_out, g_ff2, w_ff2_in, w_ff2_out):
    depth = g_ff1.shape[0]
    pos_p = jnp.arange(x_prompt.shape[1], dtype=jnp.int32)
    pos_s = PAST_LEN + jnp.arange(x_sample.shape[1], dtype=jnp.int32)
    hp, hs = x_prompt, x_sample
    outs = [[] for _ in range(6)]
    reps = V7X_LANES // HEAD_DIM
    for l in range(depth):
        w = dict(
            g_ff1=g_ff1[l][None], w_ff1_in=w_ff1_in[l].astype(BF16), w_ff1_out=w_ff1_out[l].astype(BF16),
            g_mix=g_mix[l][None], w_in=w_in[l].astype(BF16),
            g_q=jnp.tile(g_q[l], reps)[None], g_k=jnp.tile(g_k[l], reps)[None], sinks=sinks[l],
            w_dw=w_dw[l], b_dw=b_dw[l][None], g_cn=g_cn[l][None], b_cn=b_cn[l][None],
            w_out=w_out[l].astype(BF16),
            g_ff2=g_ff2[l][None], w_ff2_in=w_ff2_in[l].astype(BF16), w_ff2_out=w_ff2_out[l].astype(BF16))
        hp, cp, kp, vp = _layer(hp, pos_p, None, w)
        sb = x_sample.shape[0]
        state = (state_conv[l], cache_k_win[l].reshape(sb, WINDOW, KV_WIDTH),
                 cache_v_win[l].reshape(sb, WINDOW, KV_WIDTH))
        hs, cs, ks, vs = _layer(hs, pos_s, state, w)
        for o, val in zip(outs, (cp, kp, vp, cs, ks, vs)):
            o.append(val)
    return (hp, hs) + tuple(jnp.stack(o) for o in outs)
```

```python
import functools

import jax
import jax.numpy as jnp
from jax.experimental import pallas as pl
from jax.experimental.pallas import tpu as pltpu

D_MODEL = 1024
D_FF = 2816
CONV_CH = 512
CONV_WIDTH = 31
N_HEADS = 8
N_KV_HEADS = 2
HEAD_DIM = 64
Q_PER_KV = N_HEADS // N_KV_HEADS
ATTN_WIDTH = N_HEADS * HEAD_DIM
KV_WIDTH = N_KV_HEADS * HEAD_DIM
MIX_WIDTH = CONV_CH + ATTN_WIDTH
IN_COLS = 2 * CONV_CH + ATTN_WIDTH + 2 * KV_WIDTH
WINDOW = 128
CHUNK = 64
ROPE_DIM = HEAD_DIM // 4
ROPE_THETA = 500000.0
PAST_LEN = 4096
EPS = 1e-6

V7X_LANES = 128
V7X_SUBLANES = 8
V7X_MXU_DIM = 256
V7X_VMEM_BYTES = 64 * 1024 * 1024

HIST_ROWS = 32
HIST_OFF = HIST_ROWS - (CONV_WIDTH - 1)
FF_CHUNK = V7X_MXU_DIM
N_FF_CHUNKS = D_FF // FF_CHUNK
W_UP_SHAPE = (N_FF_CHUNKS, D_MODEL, 2 * FF_CHUNK)
W_DOWN_SHAPE = (N_FF_CHUNKS, FF_CHUNK, D_MODEL)
CONV_ROWS = 64
TOKEN_TILE = 512

F32 = jnp.float32
BF16 = jnp.bfloat16


def _dot(a, b):
    return jnp.dot(a, b, preferred_element_type=F32)


def _sigmoid(x):
    return 1.0 / (1.0 + jnp.exp(-x))


def _rms(x, g):
    return x * jax.lax.rsqrt(jnp.mean(x * x, axis=-1, keepdims=True) + EPS) * g


def _swiglu(x, g_ref, wup_ref, wdn_ref):
    xn = _rms(x, g_ref[...]).astype(BF16)
    acc = None
    for c in range(N_FF_CHUNKS):
        ab = _dot(xn, wup_ref[c])
        a = ab[:, 0:FF_CHUNK]
        b = ab[:, FF_CHUNK:2 * FF_CHUNK]
        d = _dot((a * _sigmoid(a) * b).astype(BF16), wdn_ref[c])
        acc = d if acc is None else acc + d
    return x + 0.5 * acc


def _ffn_post_kernel(x_ref, g_ref, wup_ref, wdn_ref, gmix_ref, win_ref, x1_ref, z_ref):
    x1 = _swiglu(x_ref[...], g_ref, wup_ref, wdn_ref)
    x1_ref[...] = x1
    z_ref[...] = _dot(_rms(x1, gmix_ref[...]).astype(BF16), win_ref[...])


def _ffn_pre_kernel(x_ref, mix_ref, wout_ref, g_ref, wup_ref, wdn_ref, y_ref):
    x2 = x_ref[...] + _dot(mix_ref[...], wout_ref[...])
    y_ref[...] = _swiglu(x2, g_ref, wup_ref, wdn_ref)


def _resident(shape):
    return pl.BlockSpec(shape, lambda i: (0,) * len(shape), pipeline_mode=pl.Buffered(1))


def _ffn_tile(n_tok):
    tm = min(TOKEN_TILE, n_tok)
    assert n_tok % tm == 0
    return tm


def _ffn_post_call(x, g, wup, wdn, gmix, win):
    n = x.shape[0]
    tm = _ffn_tile(n)
    row = lambda w: pl.BlockSpec((tm, w), lambda i: (i, 0))
    return pl.pallas_call(
        _ffn_post_kernel,
        grid=(n // tm,),
        in_specs=[row(D_MODEL), _resident((1, D_MODEL)), _resident(W_UP_SHAPE), _resident(W_DOWN_SHAPE),
                  _resident((1, D_MODEL)), _resident((D_MODEL, IN_COLS))],
        out_specs=[row(D_MODEL), row(IN_COLS)],
        out_shape=[jax.ShapeDtypeStruct((n, D_MODEL), F32), jax.ShapeDtypeStruct((n, IN_COLS), F32)],
        compiler_params=pltpu.CompilerParams(dimension_semantics=("arbitrary",),
                                             vmem_limit_bytes=V7X_VMEM_BYTES * 7 // 8),
        name="ffn_post",
    )(x, g, wup, wdn, gmix, win)


def _ffn_pre_call(x, mix, wout, g, wup, wdn):
    n = x.shape[0]
    tm = _ffn_tile(n)
    row = lambda w: pl.BlockSpec((tm, w), lambda i: (i, 0))
    return pl.pallas_call(
        _ffn_pre_kernel,
        grid=(n // tm,),
        in_specs=[row(D_MODEL), row(MIX_WIDTH), _resident((MIX_WIDTH, D_MODEL)), _resident((1, D_MODEL)),
                  _resident(W_UP_SHAPE), _resident(W_DOWN_SHAPE)],
        out_specs=row(D_MODEL),
        out_shape=jax.ShapeDtypeStruct((n, D_MODEL), F32),
        compiler_params=pltpu.CompilerParams(dimension_semantics=("arbitrary",),
                                             vmem_limit_bytes=V7X_VMEM_BYTES * 7 // 8),
        name="ffn_pre",
    )(x, mix, wout, g, wup, wdn)


def _head_sumsq(x):
    w = x.shape[1]
    gw = min(w, V7X_MXU_DIM)
    r = jax.lax.broadcasted_iota(jnp.int32, (gw, gw), 0) // HEAD_DIM
    c = jax.lax.broadcasted_iota(jnp.int32, (gw, gw), 1) // HEAD_DIM
    ones_bd = (r == c).astype(BF16)
    sq = x * x
    hi = sq.astype(BF16)
    lo = (sq - hi.astype(F32)).astype(BF16)
    parts = [_dot(hi[:, o:o + gw], ones_bd) + _dot(lo[:, o:o + gw], ones_bd) for o in range(0, w, gw)]
    return parts[0] if len(parts) == 1 else jnp.concatenate(parts, axis=1)


def _rope(x, cos_t, sin_t):
    lane = jax.lax.broadcasted_iota(jnp.int32, (1, V7X_LANES), 1) % HEAD_DIM
    half = ROPE_DIM // 2
    partner = jnp.where(lane < half, pltpu.roll(x, V7X_LANES - half, axis=1), pltpu.roll(x, half, axis=1))
    return x * cos_t + partner * sin_t


def _dup_heads(x):
    lane = jax.lax.broadcasted_iota(jnp.int32, (1, V7X_LANES), 1)
    sw = pltpu.roll(x, HEAD_DIM, axis=1)
    return jnp.where(lane < HEAD_DIM, x, sw), jnp.where(lane < HEAD_DIM, sw, x)


def _conv_block(ubuf, wdw_ref, bias, r0, rb):
    sub = jax.lax.broadcasted_iota(jnp.int32, (rb, 1), 0) % V7X_SUBLANES
    acc = jnp.broadcast_to(bias, (rb, CONV_CH))
    for o in range(V7X_SUBLANES):
        rows = rb if o == 0 else rb + V7X_SUBLANES
        part = None
        for kk in range((HIST_OFF + CONV_WIDTH - 1) // V7X_SUBLANES + 1):
            j = V7X_SUBLANES * kk + o - HIST_OFF
            if 0 <= j < CONV_WIDTH:
                term = ubuf[r0 + V7X_SUBLANES * kk:r0 + V7X_SUBLANES * kk + rows, :] * wdw_ref[j:j + 1, :]
                part = term if part is None else part + term
        if o == 0:
            acc = acc + part
        else:
            rolled = [pltpu.roll(part[g:g + V7X_SUBLANES, :], V7X_SUBLANES - o, axis=0)
                      for g in range(0, rows, V7X_SUBLANES)]
            acc = acc + jnp.where(sub < V7X_SUBLANES - o, jnp.concatenate(rolled[:-1], axis=0),
                                  jnp.concatenate(rolled[1:], axis=0))
    return acc


def _mixer_kernel(*refs, tm, ch, n_t, has_state):
    if has_state:
        (sinks_ref, z_ref, cos_ref, sin_ref, gq_ref, gk_ref, wdw_ref, bdw_ref, gcn_ref, bcn_ref,
         conv0_ref, k0_ref, v0_ref,
         mix_ref, cs_ref, kw_ref, vw_ref, ubuf, kbuf, vbuf, kd_buf, vd_buf, q_buf, s_buf, p_buf) = refs
    else:
        (sinks_ref, z_ref, cos_ref, sin_ref, gq_ref, gk_ref, wdw_ref, bdw_ref, gcn_ref, bcn_ref,
         mix_ref, cs_ref, kw_ref, vw_ref, ubuf, kbuf, vbuf, kd_buf, vd_buf, q_buf, s_buf, p_buf) = refs
    t = pl.program_id(1)
    win = WINDOW + ch
    n_chunks = tm // ch
    blk = Q_PER_KV * ch
    lane = jax.lax.broadcasted_iota(jnp.int32, (1, V7X_LANES), 1)

    @pl.when(t == 0)
    def _init():
        if has_state:
            ubuf[0:HIST_OFF, :] = jnp.zeros((HIST_OFF, CONV_CH), F32)
            ubuf[HIST_OFF:HIST_ROWS, :] = conv0_ref[0]
            k0 = k0_ref[0]
            v0 = v0_ref[0]
        else:
            ubuf[0:HIST_ROWS, :] = jnp.zeros((HIST_ROWS, CONV_CH), F32)
            k0 = jnp.zeros((WINDOW, KV_WIDTH), F32)
            v0 = k0
        kbuf[0:WINDOW, :] = k0
        vbuf[0:WINDOW, :] = v0
        for h, d in enumerate(_dup_heads(k0)):
            kd_buf[h, 0:WINDOW, :] = d.astype(BF16)
        for h, d in enumerate(_dup_heads(v0)):
            vd_buf[h, 0:WINDOW, :] = d.astype(BF16)

    cv = z_ref[0, :, 0:CONV_CH]
    cg = z_ref[0, :, CONV_CH:2 * CONV_CH]
    ubuf[HIST_ROWS:HIST_ROWS + tm, :] = cv * _sigmoid(cg)
    cs_ref[0] = ubuf[HIST_OFF + tm:HIST_ROWS + tm, :]

    rb = min(CONV_ROWS, tm)
    for r0 in range(0, tm, rb):
        acc = _conv_block(ubuf, wdw_ref, bdw_ref[...], r0, rb)
        mu = jnp.mean(acc, axis=-1, keepdims=True)
        xc = acc - mu
        y = xc * jax.lax.rsqrt(jnp.mean(xc * xc, axis=-1, keepdims=True) + EPS) * gcn_ref[...] + bcn_ref[...]
        mix_ref[0, r0:r0 + rb, 0:CONV_CH] = (y * _sigmoid(y)).astype(mix_ref.dtype)

    if n_t > 1:
        ubuf[HIST_OFF:HIST_ROWS, :] = ubuf[HIST_OFF + tm:HIST_ROWS + tm, :]

    cos_t = cos_ref[...]
    sin_t = sin_ref[...]
    q = z_ref[0, :, 2 * CONV_CH:2 * CONV_CH + ATTN_WIDTH]
    k = z_ref[0, :, 2 * CONV_CH + ATTN_WIDTH:2 * CONV_CH + ATTN_WIDTH + KV_WIDTH]
    v = z_ref[0, :, 2 * CONV_CH + ATTN_WIDTH + KV_WIDTH:IN_COLS]
    qn = q * jax.lax.rsqrt(_head_sumsq(q) * (1.0 / HEAD_DIM) + EPS)
    kn = k * jax.lax.rsqrt(_head_sumsq(k) * (1.0 / HEAD_DIM) + EPS) * gk_ref[...]
    kr = _rope(kn, cos_t, sin_t)
    kbuf[WINDOW:WINDOW + tm, :] = kr
    vbuf[WINDOW:WINDOW + tm, :] = v
    for h, d in enumerate(_dup_heads(kr)):
        kd_buf[h, WINDOW:WINDOW + tm, :] = d.astype(BF16)
    for h, d in enumerate(_dup_heads(v)):
        vd_buf[h, WINDOW:WINDOW + tm, :] = d.astype(BF16)
    kw_ref[0] = kbuf[tm:tm + WINDOW, :]
    vw_ref[0] = vbuf[tm:tm + WINDOW, :]
    for m in range(ATTN_WIDTH // V7X_LANES):
        qm = _rope(qn[:, m * V7X_LANES:(m + 1) * V7X_LANES] * gq_ref[...], cos_t, sin_t) * (HEAD_DIM ** -0.5)
        for p in range(2):
            q_buf[2 * m + p] = jnp.where((lane < HEAD_DIM) == (p == 0), qm, 0.0).astype(BF16)

    key_chunk = jax.lax.broadcasted_iota(jnp.int32, (1, win), 1) // CHUNK
    for i in range(n_chunks):
        r0 = i * ch
        for h in range(N_KV_HEADS):
            qs = jnp.concatenate([q_buf[Q_PER_KV * h + g, r0:r0 + ch, :] for g in range(Q_PER_KV)], axis=0)
            s = jax.lax.dot_general(qs, kd_buf[h, r0:r0 + win, :], (((1,), (1,)), ((), ())),
                                    preferred_element_type=F32)
            if not has_state:
                first_chunk = t * (tm // CHUNK) + (r0 // CHUNK) - WINDOW // CHUNK
                s = jnp.where(key_chunk + first_chunk >= 0, s, -jnp.inf)
            b0 = (i * N_KV_HEADS + h) * blk
            s_buf[b0:b0 + blk, :] = s

    sink_heads = jnp.concatenate([jnp.full((ch, 1), sinks_ref[hd], F32) for hd in range(N_HEADS)], axis=0)
    sink = jnp.concatenate([sink_heads] * n_chunks, axis=0)
    s_all = s_buf[...]
    m_row = jnp.maximum(jnp.max(s_all, axis=-1, keepdims=True), sink)
    p_all = jnp.exp(s_all - m_row)
    inv_denom = 1.0 / (jnp.sum(p_all, axis=-1, keepdims=True) + jnp.exp(sink - m_row))
    p_buf[...] = p_all.astype(BF16)

    for i in range(n_chunks):
        r0 = i * ch
        outs = []
        for h in range(N_KV_HEADS):
            b0 = (i * N_KV_HEADS + h) * blk
            o = _dot(p_buf[b0:b0 + blk, :], vd_buf[h, r0:r0 + win, :]) * inv_denom[b0:b0 + blk, :]
            for g in range(0, Q_PER_KV, 2):
                outs.append(jnp.where(lane < HEAD_DIM, o[g * ch:(g + 1) * ch, :], o[(g + 1) * ch:(g + 2) * ch, :]))
        mix_ref[0, r0:r0 + ch, CONV_CH:MIX_WIDTH] = jnp.concatenate(outs, axis=1).astype(mix_ref.dtype)

    if n_t > 1:
        kbuf[0:WINDOW, :] = kbuf[tm:tm + WINDOW, :]
        vbuf[0:WINDOW, :] = vbuf[tm:tm + WINDOW, :]
        for h in range(N_KV_HEADS):
            kd_buf[h, 0:WINDOW, :] = kd_buf[h, tm:tm + WINDOW, :]
            vd_buf[h, 0:WINDOW, :] = vd_buf[h, tm:tm + WINDOW, :]


def _mixer_call(z, cos_t, sin_t, sinks, gq, gk, wdw, bdw, gcn, bcn, state=None):
    bsz, t_len, _ = z.shape
    has_state = state is not None
    tm = min(TOKEN_TILE, t_len)
    ch = min(CHUNK, tm)
    assert t_len % tm == 0 and tm % ch == 0
    if has_state:
        assert t_len == tm == ch
    else:
        assert ch == CHUNK and tm >= WINDOW
    n_t = t_len // tm
    score_rows = (tm // ch) * N_KV_HEADS * Q_PER_KV * ch
    full = lambda shape: pl.BlockSpec(shape, lambda b, t: (0,) * len(shape))
    per_b = lambda r, w: pl.BlockSpec((1, r, w), lambda b, t: (b, 0, 0))
    in_specs = [pl.BlockSpec(memory_space=pltpu.SMEM),
                pl.BlockSpec((1, tm, IN_COLS), lambda b, t: (b, t, 0)),
                pl.BlockSpec((tm, V7X_LANES), lambda b, t: (t, 0)),
                pl.BlockSpec((tm, V7X_LANES), lambda b, t: (t, 0)),
                full((1, V7X_LANES)), full((1, V7X_LANES)), full((CONV_WIDTH, CONV_CH)),
                full((1, CONV_CH)), full((1, CONV_CH)), full((1, CONV_CH))]
    args = [sinks, z, cos_t, sin_t, gq, gk, wdw, bdw, gcn, bcn]
    if has_state:
        in_specs += [per_b(CONV_WIDTH - 1, CONV_CH), per_b(WINDOW, KV_WIDTH), per_b(WINDOW, KV_WIDTH)]
        args += list(state)
    kern = functools.partial(_mixer_kernel, tm=tm, ch=ch, n_t=n_t, has_state=has_state)
    return pl.pallas_call(
        kern,
        grid=(bsz, n_t),
        in_specs=in_specs,
        out_specs=[pl.BlockSpec((1, tm, MIX_WIDTH), lambda b, t: (b, t, 0)),
                   per_b(CONV_WIDTH - 1, CONV_CH), per_b(WINDOW, KV_WIDTH), per_b(WINDOW, KV_WIDTH)],
        out_shape=[jax.ShapeDtypeStruct((bsz, t_len, MIX_WIDTH), BF16),
                   jax.ShapeDtypeStruct((bsz, CONV_WIDTH - 1, CONV_CH), F32),
                   jax.ShapeDtypeStruct((bsz, WINDOW, KV_WIDTH), F32),
                   jax.ShapeDtypeStruct((bsz, WINDOW, KV_WIDTH), F32)],
        scratch_shapes=[pltpu.VMEM((HIST_ROWS + tm, CONV_CH), F32),
                        pltpu.VMEM((WINDOW + tm, KV_WIDTH), F32),
                        pltpu.VMEM((WINDOW + tm, KV_WIDTH), F32),
                        pltpu.VMEM((N_KV_HEADS, WINDOW + tm, KV_WIDTH), BF16),
                        pltpu.VMEM((N_KV_HEADS, WINDOW + tm, KV_WIDTH), BF16),
                        pltpu.VMEM((N_HEADS, tm, V7X_LANES), BF16),
                        pltpu.VMEM((score_rows, WINDOW + ch), F32),
                        pltpu.VMEM((score_rows, WINDOW + ch), BF16)],
        compiler_params=pltpu.CompilerParams(dimension_semantics=("arbitrary", "arbitrary"),
                                             vmem_limit_bytes=V7X_VMEM_BYTES * 3 // 4),
        name="mixer_state" if has_state else "mixer",
    )(*args)


def _rope_tables(pos):
    half = ROPE_DIM // 2
    inv = jnp.power(jnp.float32(ROPE_THETA), -jnp.arange(half, dtype=F32) / half)
    ang = pos.astype(F32)[:, None] * inv[None, :]
    cos, sin = jnp.cos(ang), jnp.sin(ang)
    rest = HEAD_DIM - ROPE_DIM
    n = pos.shape[0]
    cos_h = jnp.concatenate([cos, cos, jnp.ones((n, rest), F32)], axis=1)
    sin_h = jnp.concatenate([-sin, sin, jnp.zeros((n, rest), F32)], axis=1)
    reps = V7X_LANES // HEAD_DIM
    return jnp.tile(cos_h, (1, reps)), jnp.tile(sin_h, (1, reps))


def _chunk_ffn_weights(w_up, w_down):
    a = w_up[:, :D_FF].reshape(D_MODEL, N_FF_CHUNKS, FF_CHUNK)
    b = w_up[:, D_FF:].reshape(D_MODEL, N_FF_CHUNKS, FF_CHUNK)
    up = jnp.concatenate([a, b], axis=2).transpose(1, 0, 2)
    return up.astype(BF16), w_down.reshape(N_FF_CHUNKS, FF_CHUNK, D_MODEL).astype(BF16)


def _layer(x, pos, state, w):
    bsz, t_len, _ = x.shape
    x2d = x.reshape(bsz * t_len, D_MODEL)
    x1, z = _ffn_post_call(x2d, w["g_ff1"], w["w_ff1_in"], w["w_ff1_out"], w["g_mix"], w["w_in"])
    cos_t, sin_t = _rope_tables(pos)
    mix, cs, kw, vw = _mixer_call(z.reshape(bsz, t_len, IN_COLS), cos_t, sin_t, w["sinks"], w["g_q"], w["g_k"],
                                  w["w_dw"], w["b_dw"], w["g_cn"], w["b_cn"], state)
    y = _ffn_pre_call(x1, mix.reshape(bsz * t_len, MIX_WIDTH), w["w_out"], w["g_ff2"], w["w_ff2_in"], w["w_ff2_out"])
    kv_shape = (bsz, WINDOW, N_KV_HEADS, HEAD_DIM)
    return y.reshape(bsz, t_len, D_MODEL), cs, kw.reshape(kv_shape), vw.reshape(kv_shape)


def kernel(x_prompt, x_sample, state_conv, cache_k_win, cache_v_win, g_ff1, w_ff1_in, w_ff1_out, g_mix, w_in, g_q, g_k, sinks, w_dw, b_dw, g_cn, b_cn, w_out, g_ff2, w_ff2_in, w_ff2_out):
    depth = g_ff1.shape[0]
    pos_p = jnp.arange(x_prompt.shape[1], dtype=jnp.int32)
    pos_s = PAST_LEN + jnp.arange(x_sample.shape[1], dtype=jnp.int32)
    hp, hs = x_prompt, x_sample
    outs = [[] for _ in range(6)]
    reps = V7X_LANES // HEAD_DIM
    for l in range(depth):
        up1, down1 = _chunk_ffn_weights(w_ff1_in[l], w_ff1_out[l])
        up2, down2 = _chunk_ffn_weights(w_ff2_in[l], w_ff2_out[l])
        w = dict(
            g_ff1=g_ff1[l][None], w_ff1_in=up1, w_ff1_out=down1,
            g_mix=g_mix[l][None], w_in=w_in[l].astype(BF16),
            g_q=jnp.tile(g_q[l], reps)[None], g_k=jnp.tile(g_k[l], reps)[None], sinks=sinks[l],
            w_dw=w_dw[l], b_dw=b_dw[l][None], g_cn=g_cn[l][None], b_cn=b_cn[l][None],
            w_out=w_out[l].astype(BF16),
            g_ff2=g_ff2[l][None], w_ff2_in=up2, w_ff2_out=down2)
        hp, cp, kp, vp = _layer(hp, pos_p, None, w)
        sb = x_sample.shape[0]
        state = (state_conv[l], cache_k_win[l].reshape(sb, WINDOW, KV_WIDTH),
                 cache_v_win[l].reshape(sb, WINDOW, KV_WIDTH))
        hs, cs, ks, vs = _layer(hs, pos_s, state, w)
        for o, val in zip(outs, (cp, kp, vp, cs, ks, vs)):
            o.append(val)
    return (hp, hs) + tuple(jnp.stack(o) for o in outs)
```

```python
import functools

import jax
import jax.numpy as jnp
from jax.experimental import pallas as pl
from jax.experimental.pallas import tpu as pltpu

D_MODEL = 1024
D_FF = 2816
CONV_CH = 512
CONV_WIDTH = 31
N_HEADS = 8
N_KV_HEADS = 2
HEAD_DIM = 64
Q_PER_KV = N_HEADS // N_KV_HEADS
ATTN_WIDTH = N_HEADS * HEAD_DIM
KV_WIDTH = N_KV_HEADS * HEAD_DIM
MIX_WIDTH = CONV_CH + ATTN_WIDTH
IN_COLS = 2 * CONV_CH + ATTN_WIDTH + 2 * KV_WIDTH
WINDOW = 128
CHUNK = 64
ROPE_DIM = HEAD_DIM // 4
ROPE_THETA = 500000.0
PAST_LEN = 4096
EPS = 1e-6

V7X_LANES = 128
V7X_SUBLANES = 8
V7X_MXU_DIM = 256
V7X_VMEM_BYTES = 64 * 1024 * 1024

HIST_ROWS = 32
HIST_OFF = HIST_ROWS - (CONV_WIDTH - 1)
FF_CHUNK = V7X_MXU_DIM
N_FF_CHUNKS = D_FF // FF_CHUNK
W_UP_SHAPE = (D_MODEL, 2 * D_FF)
W_DOWN_SHAPE = (D_FF, D_MODEL)
CONV_ROWS = 128
TOKEN_TILE = 512
FFN_PRE_TILE = 1024

F32 = jnp.float32
BF16 = jnp.bfloat16


def _dot(a, b):
    return jnp.dot(a, b, preferred_element_type=F32)


def _sigmoid(x):
    return 1.0 / (1.0 + jnp.exp(-x))


def _rms(x, g):
    return x * jax.lax.rsqrt(jnp.mean(x * x, axis=-1, keepdims=True) + EPS) * g


def _swiglu(x, g_ref, wup_ref, wdn_ref):
    xn = _rms(x, g_ref[...]).astype(BF16)
    acc = None
    for c in range(N_FF_CHUNKS):
        lo = c * FF_CHUNK
        a = _dot(xn, wup_ref[:, lo:lo + FF_CHUNK])
        b = _dot(xn, wup_ref[:, D_FF + lo:D_FF + lo + FF_CHUNK])
        d = _dot((a * _sigmoid(a) * b).astype(BF16), wdn_ref[lo:lo + FF_CHUNK, :])
        acc = d if acc is None else acc + d
    return x + 0.5 * acc


def _ffn_post_kernel(x_ref, g_ref, wup_ref, wdn_ref, gmix_ref, win_ref, x1_ref, z_ref):
    x1 = _swiglu(x_ref[...], g_ref, wup_ref, wdn_ref)
    x1_ref[...] = x1
    z_ref[...] = _dot(_rms(x1, gmix_ref[...]).astype(BF16), win_ref[...])


def _ffn_pre_kernel(x_ref, mix_ref, wout_ref, g_ref, wup_ref, wdn_ref, y_ref):
    x2 = x_ref[...] + _dot(mix_ref[...], wout_ref[...])
    y_ref[...] = _swiglu(x2, g_ref, wup_ref, wdn_ref)


def _resident(shape):
    return pl.BlockSpec(shape, lambda i: (0,) * len(shape), pipeline_mode=pl.Buffered(1))


def _ffn_tile(n_tok, tile=TOKEN_TILE):
    tm = min(tile, n_tok)
    assert n_tok % tm == 0
    return tm


def _ffn_post_call(x, g, wup, wdn, gmix, win):
    n = x.shape[0]
    tm = _ffn_tile(n)
    row = lambda w: pl.BlockSpec((tm, w), lambda i: (i, 0))
    return pl.pallas_call(
        _ffn_post_kernel,
        grid=(n // tm,),
        in_specs=[row(D_MODEL), _resident((1, D_MODEL)), _resident(W_UP_SHAPE), _resident(W_DOWN_SHAPE),
                  _resident((1, D_MODEL)), _resident((D_MODEL, IN_COLS))],
        out_specs=[row(D_MODEL), row(IN_COLS)],
        out_shape=[jax.ShapeDtypeStruct((n, D_MODEL), F32), jax.ShapeDtypeStruct((n, IN_COLS), F32)],
        compiler_params=pltpu.CompilerParams(dimension_semantics=("arbitrary",),
                                             vmem_limit_bytes=V7X_VMEM_BYTES * 7 // 8),
        name="ffn_post",
    )(x, g, wup, wdn, gmix, win)


def _ffn_pre_call(x, mix, wout, g, wup, wdn):
    n = x.shape[0]
    tm = _ffn_tile(n, FFN_PRE_TILE)
    row = lambda w: pl.BlockSpec((tm, w), lambda i: (i, 0))
    return pl.pallas_call(
        _ffn_pre_kernel,
        grid=(n // tm,),
        in_specs=[row(D_MODEL), row(MIX_WIDTH), _resident((MIX_WIDTH, D_MODEL)), _resident((1, D_MODEL)),
                  _resident(W_UP_SHAPE), _resident(W_DOWN_SHAPE)],
        out_specs=row(D_MODEL),
        out_shape=jax.ShapeDtypeStruct((n, D_MODEL), F32),
        compiler_params=pltpu.CompilerParams(dimension_semantics=("arbitrary",),
                                             vmem_limit_bytes=V7X_VMEM_BYTES * 7 // 8),
        name="ffn_pre",
    )(x, mix, wout, g, wup, wdn)


def _head_sumsq(x):
    w = x.shape[1]
    gw = min(w, V7X_MXU_DIM)
    r = jax.lax.broadcasted_iota(jnp.int32, (gw, gw), 0) // HEAD_DIM
    c = jax.lax.broadcasted_iota(jnp.int32, (gw, gw), 1) // HEAD_DIM
    ones_bd = (r == c).astype(BF16)
    sq = x * x
    hi = sq.astype(BF16)
    lo = (sq - hi.astype(F32)).astype(BF16)
    parts = [_dot(hi[:, o:o + gw], ones_bd) + _dot(lo[:, o:o + gw], ones_bd) for o in range(0, w, gw)]
    return parts[0] if len(parts) == 1 else jnp.concatenate(parts, axis=1)


def _rope(x, cos_t, sin_t):
    lane = jax.lax.broadcasted_iota(jnp.int32, (1, V7X_LANES), 1) % HEAD_DIM
    half = ROPE_DIM // 2
    partner = jnp.where(lane < half, pltpu.roll(x, V7X_LANES - half, axis=1), pltpu.roll(x, half, axis=1))
    return x * cos_t + partner * sin_t


def _dup_heads(x):
    lane = jax.lax.broadcasted_iota(jnp.int32, (1, V7X_LANES), 1)
    sw = pltpu.roll(x, HEAD_DIM, axis=1)
    return jnp.where(lane < HEAD_DIM, x, sw), jnp.where(lane < HEAD_DIM, sw, x)


def _conv_block(ubuf, wdw_ref, bias, r0, rb):
    sub = jax.lax.broadcasted_iota(jnp.int32, (rb, 1), 0) % V7X_SUBLANES
    acc = jnp.broadcast_to(bias, (rb, CONV_CH))
    for o in range(V7X_SUBLANES):
        rows = rb if o == 0 else rb + V7X_SUBLANES
        part = None
        for kk in range((HIST_OFF + CONV_WIDTH - 1) // V7X_SUBLANES + 1):
            j = V7X_SUBLANES * kk + o - HIST_OFF
            if 0 <= j < CONV_WIDTH:
                w_j = wdw_ref[V7X_SUBLANES * j:V7X_SUBLANES * (j + 1), :]
                u_j = ubuf[r0 + V7X_SUBLANES * kk:r0 + V7X_SUBLANES * kk + rows, :]
                term = (u_j.reshape(rows // V7X_SUBLANES, V7X_SUBLANES, CONV_CH) * w_j[None]).reshape(rows, CONV_CH)
                part = term if part is None else part + term
        if o == 0:
            acc = acc + part
        else:
            rolled = [pltpu.roll(part[g:g + V7X_SUBLANES, :], V7X_SUBLANES - o, axis=0)
                      for g in range(0, rows, V7X_SUBLANES)]
            acc = acc + jnp.where(sub < V7X_SUBLANES - o, jnp.concatenate(rolled[:-1], axis=0),
                                  jnp.concatenate(rolled[1:], axis=0))
    return acc


def _mixer_kernel(*refs, tm, ch, n_t, has_state):
    if has_state:
        (sinks_ref, z_ref, cos_ref, sin_ref, gq_ref, gk_ref, wdw_ref, bdw_ref, gcn_ref, bcn_ref,
         conv0_ref, k0_ref, v0_ref,
         mix_ref, cs_ref, kw_ref, vw_ref, ubuf, kbuf, vbuf, kd_buf, vd_buf, q_buf, s_buf, p_buf) = refs
    else:
        (sinks_ref, z_ref, cos_ref, sin_ref, gq_ref, gk_ref, wdw_ref, bdw_ref, gcn_ref, bcn_ref,
         mix_ref, cs_ref, kw_ref, vw_ref, ubuf, kbuf, vbuf, kd_buf, vd_buf, q_buf, s_buf, p_buf) = refs
    t = pl.program_id(1)
    win = WINDOW + ch
    n_chunks = tm // ch
    blk = Q_PER_KV * ch
    lane = jax.lax.broadcasted_iota(jnp.int32, (1, V7X_LANES), 1)

    @pl.when(t == 0)
    def _init():
        if has_state:
            ubuf[0:HIST_OFF, :] = jnp.zeros((HIST_OFF, CONV_CH), F32)
            ubuf[HIST_OFF:HIST_ROWS, :] = conv0_ref[0]
            k0 = k0_ref[0]
            v0 = v0_ref[0]
        else:
            ubuf[0:HIST_ROWS, :] = jnp.zeros((HIST_ROWS, CONV_CH), F32)
            k0 = jnp.zeros((WINDOW, KV_WIDTH), F32)
            v0 = k0
        kbuf[0:WINDOW, :] = k0
        vbuf[0:WINDOW, :] = v0
        for h, d in enumerate(_dup_heads(k0)):
            kd_buf[h, 0:WINDOW, :] = d.astype(BF16)
        for h, d in enumerate(_dup_heads(v0)):
            vd_buf[h, 0:WINDOW, :] = d.astype(BF16)

    cv = z_ref[0, :, 0:CONV_CH]
    cg = z_ref[0, :, CONV_CH:2 * CONV_CH]
    ubuf[HIST_ROWS:HIST_ROWS + tm, :] = cv * _sigmoid(cg)
    cs_ref[0] = ubuf[HIST_OFF + tm:HIST_ROWS + tm, :]

    rb = min(CONV_ROWS, tm)
    for r0 in range(0, tm, rb):
        acc = _conv_block(ubuf, wdw_ref, bdw_ref[...], r0, rb)
        mu = jnp.mean(acc, axis=-1, keepdims=True)
        xc = acc - mu
        y = xc * jax.lax.rsqrt(jnp.mean(xc * xc, axis=-1, keepdims=True) + EPS) * gcn_ref[...] + bcn_ref[...]
        mix_ref[0, r0:r0 + rb, 0:CONV_CH] = (y * _sigmoid(y)).astype(mix_ref.dtype)

    if n_t > 1:
        ubuf[HIST_OFF:HIST_ROWS, :] = ubuf[HIST_OFF + tm:HIST_ROWS + tm, :]

    cos_t = cos_ref[...]
    sin_t = sin_ref[...]
    q = z_ref[0, :, 2 * CONV_CH:2 * CONV_CH + ATTN_WIDTH]
    k = z_ref[0, :, 2 * CONV_CH + ATTN_WIDTH:2 * CONV_CH + ATTN_WIDTH + KV_WIDTH]
    v = z_ref[0, :, 2 * CONV_CH + ATTN_WIDTH + KV_WIDTH:IN_COLS]
    qn = q * jax.lax.rsqrt(_head_sumsq(q) * (1.0 / HEAD_DIM) + EPS)
    kn = k * jax.lax.rsqrt(_head_sumsq(k) * (1.0 / HEAD_DIM) + EPS) * gk_ref[...]
    kr = _rope(kn, cos_t, sin_t)
    kbuf[WINDOW:WINDOW + tm, :] = kr
    vbuf[WINDOW:WINDOW + tm, :] = v
    for h, d in enumerate(_dup_heads(kr)):
        kd_buf[h, WINDOW:WINDOW + tm, :] = d.astype(BF16)
    for h, d in enumerate(_dup_heads(v)):
        vd_buf[h, WINDOW:WINDOW + tm, :] = d.astype(BF16)
    kw_ref[0] = kbuf[tm:tm + WINDOW, :]
    vw_ref[0] = vbuf[tm:tm + WINDOW, :]
    for m in range(ATTN_WIDTH // V7X_LANES):
        qm = _rope(qn[:, m * V7X_LANES:(m + 1) * V7X_LANES] * gq_ref[...], cos_t, sin_t) * (HEAD_DIM ** -0.5)
        for p in range(2):
            q_buf[2 * m + p] = jnp.where((lane < HEAD_DIM) == (p == 0), qm, 0.0).astype(BF16)

    key_chunk = jax.lax.broadcasted_iota(jnp.int32, (1, win), 1) // CHUNK
    for i in range(n_chunks):
        r0 = i * ch
        for h in range(N_KV_HEADS):
            qs = jnp.concatenate([q_buf[Q_PER_KV * h + g, r0:r0 + ch, :] for g in range(Q_PER_KV)], axis=0)
            s = jax.lax.dot_general(qs, kd_buf[h, r0:r0 + win, :], (((1,), (1,)), ((), ())),
                                    preferred_element_type=F32)
            if not has_state:
                first_chunk = t * (tm // CHUNK) + (r0 // CHUNK) - WINDOW // CHUNK
                s = jnp.where(key_chunk + first_chunk >= 0, s, -jnp.inf)
            b0 = (i * N_KV_HEADS + h) * blk
            s_buf[b0:b0 + blk, :] = s

    sink_heads = jnp.concatenate([jnp.full((ch, 1), sinks_ref[hd], F32) for hd in range(N_HEADS)], axis=0)
    sink = jnp.concatenate([sink_heads] * n_chunks, axis=0)
    s_all = s_buf[...]
    m_row = jnp.maximum(jnp.max(s_all, axis=-1, keepdims=True), sink)
    p_all = jnp.exp(s_all - m_row)
    inv_denom = 1.0 / (jnp.sum(p_all, axis=-1, keepdims=True) + jnp.exp(sink - m_row))
    p_buf[...] = p_all.astype(BF16)

    for i in range(n_chunks):
        r0 = i * ch
        outs = []
        for h in range(N_KV_HEADS):
            b0 = (i * N_KV_HEADS + h) * blk
            o = _dot(p_buf[b0:b0 + blk, :], vd_buf[h, r0:r0 + win, :]) * inv_denom[b0:b0 + blk, :]
            for g in range(0, Q_PER_KV, 2):
                outs.append(jnp.where(lane < HEAD_DIM, o[g * ch:(g + 1) * ch, :], o[(g + 1) * ch:(g + 2) * ch, :]))
        mix_ref[0, r0:r0 + ch, CONV_CH:MIX_WIDTH] = jnp.concatenate(outs, axis=1).astype(mix_ref.dtype)

    if n_t > 1:
        kbuf[0:WINDOW, :] = kbuf[tm:tm + WINDOW, :]
        vbuf[0:WINDOW, :] = vbuf[tm:tm + WINDOW, :]
        for h in range(N_KV_HEADS):
            kd_buf[h, 0:WINDOW, :] = kd_buf[h, tm:tm + WINDOW, :]
            vd_buf[h, 0:WINDOW, :] = vd_buf[h, tm:tm + WINDOW, :]


def _mixer_call(z, cos_t, sin_t, sinks, gq, gk, wdw, bdw, gcn, bcn, state=None):
    bsz, t_len, _ = z.shape
    has_state = state is not None
    tm = min(TOKEN_TILE, t_len)
    ch = min(CHUNK, tm)
    assert t_len % tm == 0 and tm % ch == 0
    if has_state:
        assert t_len == tm == ch
    else:
        assert ch == CHUNK and tm >= WINDOW
    n_t = t_len // tm
    score_rows = (tm // ch) * N_KV_HEADS * Q_PER_KV * ch
    full = lambda shape: pl.BlockSpec(shape, lambda b, t: (0,) * len(shape))
    per_b = lambda r, w: pl.BlockSpec((1, r, w), lambda b, t: (b, 0, 0))
    in_specs = [pl.BlockSpec(memory_space=pltpu.SMEM),
                pl.BlockSpec((1, tm, IN_COLS), lambda b, t: (b, t, 0)),
                pl.BlockSpec((tm, V7X_LANES), lambda b, t: (t, 0)),
                pl.BlockSpec((tm, V7X_LANES), lambda b, t: (t, 0)),
                full((1, V7X_LANES)), full((1, V7X_LANES)), full((CONV_WIDTH * V7X_SUBLANES, CONV_CH)),
                full((1, CONV_CH)), full((1, CONV_CH)), full((1, CONV_CH))]
    args = [sinks, z, cos_t, sin_t, gq, gk, wdw, bdw, gcn, bcn]
    if has_state:
        in_specs += [per_b(CONV_WIDTH - 1, CONV_CH), per_b(WINDOW, KV_WIDTH), per_b(WINDOW, KV_WIDTH)]
        args += list(state)
    kern = functools.partial(_mixer_kernel, tm=tm, ch=ch, n_t=n_t, has_state=has_state)
    return pl.pallas_call(
        kern,
        grid=(bsz, n_t),
        in_specs=in_specs,
        out_specs=[pl.BlockSpec((1, tm, MIX_WIDTH), lambda b, t: (b, t, 0)),
                   per_b(CONV_WIDTH - 1, CONV_CH), per_b(WINDOW, KV_WIDTH), per_b(WINDOW, KV_WIDTH)],
        out_shape=[jax.ShapeDtypeStruct((bsz, t_len, MIX_WIDTH), BF16),
                   jax.ShapeDtypeStruct((bsz, CONV_WIDTH - 1, CONV_CH), F32),
                   jax.ShapeDtypeStruct((bsz, WINDOW, KV_WIDTH), F32),
                   jax.ShapeDtypeStruct((bsz, WINDOW, KV_WIDTH), F32)],
        scratch_shapes=[pltpu.VMEM((HIST_ROWS + tm, CONV_CH), F32),
                        pltpu.VMEM((WINDOW + tm, KV_WIDTH), F32),
                        pltpu.VMEM((WINDOW + tm, KV_WIDTH), F32),
                        pltpu.VMEM((N_KV_HEADS, WINDOW + tm, KV_WIDTH), BF16),
                        pltpu.VMEM((N_KV_HEADS, WINDOW + tm, KV_WIDTH), BF16),
                        pltpu.VMEM((N_HEADS, tm, V7X_LANES), BF16),
                        pltpu.VMEM((score_rows, WINDOW + ch), F32),
                        pltpu.VMEM((score_rows, WINDOW + ch), BF16)],
        compiler_params=pltpu.CompilerParams(dimension_semantics=("arbitrary", "arbitrary"),
                                             vmem_limit_bytes=V7X_VMEM_BYTES * 3 // 4),
        name="mixer_state" if has_state else "mixer",
    )(*args)


def _rope_tables(pos):
    half = ROPE_DIM // 2
    inv = jnp.power(jnp.float32(ROPE_THETA), -jnp.arange(half, dtype=F32) / half)
    ang = pos.astype(F32)[:, None] * inv[None, :]
    cos, sin = jnp.cos(ang), jnp.sin(ang)
    rest = HEAD_DIM - ROPE_DIM
    n = pos.shape[0]
    cos_h = jnp.concatenate([cos, cos, jnp.ones((n, rest), F32)], axis=1)
    sin_h = jnp.concatenate([-sin, sin, jnp.zeros((n, rest), F32)], axis=1)
    reps = V7X_LANES // HEAD_DIM
    return jnp.tile(cos_h, (1, reps)), jnp.tile(sin_h, (1, reps))


def _layer(x, pos, state, w):
    bsz, t_len, _ = x.shape
    x2d = x.reshape(bsz * t_len, D_MODEL)
    x1, z = _ffn_post_call(x2d, w["g_ff1"], w["w_ff1_in"], w["w_ff1_out"], w["g_mix"], w["w_in"])
    cos_t, sin_t = _rope_tables(pos)
    mix, cs, kw, vw = _mixer_call(z.reshape(bsz, t_len, IN_COLS), cos_t, sin_t, w["sinks"], w["g_q"], w["g_k"],
                                  w["w_dw"], w["b_dw"], w["g_cn"], w["b_cn"], state)
    y = _ffn_pre_call(x1, mix.reshape(bsz * t_len, MIX_WIDTH), w["w_out"], w["g_ff2"], w["w_ff2_in"], w["w_ff2_out"])
    kv_shape = (bsz, WINDOW, N_KV_HEADS, HEAD_DIM)
    return y.reshape(bsz, t_len, D_MODEL), cs, kw.reshape(kv_shape), vw.reshape(kv_shape)


def kernel(x_prompt, x_sample, state_conv, cache_k_win, cache_v_win, g_ff1, w_ff1_in, w_ff1_out, g_mix, w_in, g_q, g_k, sinks, w_dw, b_dw, g_cn, b_cn, w_out, g_ff2, w_ff2_in, w_ff2_out):
    depth = g_ff1.shape[0]
    pos_p = jnp.arange(x_prompt.shape[1], dtype=jnp.int32)
    pos_s = PAST_LEN + jnp.arange(x_sample.shape[1], dtype=jnp.int32)
    hp, hs = x_prompt, x_sample
    outs = [[] for _ in range(6)]
    reps = V7X_LANES // HEAD_DIM
    for l in range(depth):
        w = dict(
            g_ff1=g_ff1[l][None], w_ff1_in=w_ff1_in[l].astype(BF16), w_ff1_out=w_ff1_out[l].astype(BF16),
            g_mix=g_mix[l][None], w_in=w_in[l].astype(BF16),
            g_q=jnp.tile(g_q[l], reps)[None], g_k=jnp.tile(g_k[l], reps)[None], sinks=sinks[l],
            w_dw=jnp.repeat(w_dw[l], V7X_SUBLANES, axis=0),
            b_dw=b_dw[l][None], g_cn=g_cn[l][None], b_cn=b_cn[l][None],
            w_out=w_out[l].astype(BF16),
            g_ff2=g_ff2[l][None], w_ff2_in=w_ff2_in[l].astype(BF16), w_ff2_out=w_ff2_out[l].astype(BF16))
        hp, cp, kp, vp = _layer(hp, pos_p, None, w)
        sb = x_sample.shape[0]
        state = (state_conv[l], cache_k_win[l].reshape(sb, WINDOW, KV_WIDTH),
                 cache_v_win[l].reshape(sb, WINDOW, KV_WIDTH))
        hs, cs, ks, vs = _layer(hs, pos_s, state, w)
        for o, val in zip(outs, (cp, kp, vp, cs, ks, vs)):
            o.append(val)
    return (hp, hs) + tuple(jnp.stack(o) for o in outs)
```
